```python
import math
import jax, jax.numpy as jnp
from jax import lax
import numpy as np

D_MODEL = 1024
BATCH = 8
SEQ = 2048
DEPTH = 4

N_AB = (DEPTH + 1) // 2
N_RET = DEPTH // 2

CONV_GROUPS = 8
CONV_WIDTH = D_MODEL // 2
CONV_K = 3
MLA_HEADS = 8
MLA_NOPE = 64
MLA_ROPE = 32
MLA_V = 64
MLA_Q_RANK = 256
MLA_KV_RANK = 128
MLA_WIDTH = MLA_HEADS * MLA_V
Q_BLOCK = 128
ROPE_BASE = 10000.0
AB_IN = 3 * CONV_WIDTH + MLA_Q_RANK + MLA_KV_RANK + MLA_ROPE
AB_OUT = CONV_WIDTH + MLA_WIDTH
AB_SPLITS = (CONV_WIDTH, 2 * CONV_WIDTH, 3 * CONV_WIDTH,
             3 * CONV_WIDTH + MLA_Q_RANK, 3 * CONV_WIDTH + MLA_Q_RANK + MLA_KV_RANK)
RET_HEADS = 8
RET_DK = D_MODEL // RET_HEADS
RET_DV = 2 * RET_DK
RET_CHUNK = 128
RET_IN = 2 * RET_HEADS * RET_DK + 2 * RET_HEADS * RET_DV
RET_SPLITS = (RET_HEADS * RET_DK, 2 * RET_HEADS * RET_DK, 2 * RET_HEADS * RET_DK + RET_HEADS * RET_DV)
N_GROUPS = 4
EXPERTS_PER_GROUP = 8
N_EXPERTS = N_GROUPS * EXPERTS_PER_GROUP
TOP_K = 2
D_EXPERT = 256
LN_EPS = 1e-5
RMS_EPS = 1e-6
DEEPNORM_ALPHA = (2 * DEPTH) ** 0.25
DEEPNORM_BETA = (8 * DEPTH) ** -0.25

kernel_name = 'hybrid_conv_mla_retention_hmoe'


def layer_norm(x, g, b):
    xf = x.astype(jnp.float32)
    mu = jnp.mean(xf, axis=-1, keepdims=True)
    var = jnp.mean(jnp.square(xf - mu), axis=-1, keepdims=True)
    y = (xf - mu) * lax.rsqrt(var + LN_EPS)
    return (y * g.astype(jnp.float32) + b.astype(jnp.float32)).astype(x.dtype)


def rms_norm(x, g):
    xf = x.astype(jnp.float32)
    y = xf * lax.rsqrt(jnp.mean(jnp.square(xf), axis=-1, keepdims=True) + RMS_EPS)
    return (y * g.astype(jnp.float32)).astype(x.dtype)


def rope_tables(positions, dim):
    inv = ROPE_BASE ** (-jnp.arange(0, dim, 2, dtype=jnp.float32) / dim)
    ang = positions.astype(jnp.float32)[..., None] * inv
    return jnp.cos(ang), jnp.sin(ang)


def apply_rope(x, cos, sin):
    xf = x.astype(jnp.float32)
    half = x.shape[-1] // 2
    x1, x2 = xf[..., :half], xf[..., half:]
    c, s = cos[:, :, None, :], sin[:, :, None, :]
    return jnp.concatenate([x1 * c - x2 * s, x2 * c + x1 * s], axis=-1).astype(x.dtype)


def adaln(c, w, b):
    m = jax.nn.silu(c) @ w + b
    shift, scale, gate = jnp.split(m, 3, axis=-1)
    return shift, scale, gate


def modulate(x, shift, scale):
    return x * (1.0 + scale[:, None, :]) + shift[:, None, :]


def short_conv_mix(a_x, a_b, a_c, conv_w):
    u = a_c * a_x
    s = u.shape[1]
    up = jnp.pad(u, ((0, 0), (CONV_K - 1, 0), (0, 0)))
    y = conv_w[0] * up[:, 0:s]
    for j in range(1, CONV_K):
        y = y + conv_w[j] * up[:, j:j + s]
    return a_b * y


def causal_attention_blocks(q, k, v):
    b, s, h, dqk = q.shape
    nb = s // Q_BLOCK
    scale = dqk ** -0.5
    qb = q.reshape(b, nb, Q_BLOCK, h, dqk).transpose(1, 0, 2, 3, 4)
    k_idx = jnp.arange(s)

    def one_block(args):
        i, qi = args
        sc = jnp.einsum('bqhd,bkhd->bhqk', qi, k).astype(jnp.float32) * scale
        q_idx = i * Q_BLOCK + jnp.arange(Q_BLOCK)
        sc = jnp.where(k_idx[None, :] <= q_idx[:, None], sc, -jnp.inf)
        p = jax.nn.softmax(sc, axis=-1).astype(v.dtype)
        return jnp.einsum('bhqk,bkhd->bqhd', p, v)

    o = lax.map(one_block, (jnp.arange(nb), qb))
    return o.transpose(1, 0, 2, 3, 4).reshape(b, s, h, v.shape[-1])


def mla_mix(q_c, kv_c, k_r, q_norm_g, kv_norm_g, w_uq, w_ukv, cos, sin):
    b, s, _ = q_c.shape
    q = (rms_norm(q_c, q_norm_g) @ w_uq).reshape(b, s, MLA_HEADS, MLA_NOPE + MLA_ROPE)
    q = jnp.concatenate([q[..., :MLA_NOPE], apply_rope(q[..., MLA_NOPE:], cos, sin)], axis=-1)
    kv = (rms_norm(kv_c, kv_norm_g) @ w_ukv).reshape(b, s, MLA_HEADS, MLA_NOPE + MLA_V)
    k_nope, v = kv[..., :MLA_NOPE], kv[..., MLA_NOPE:]
    k_rope = apply_rope(k_r[:, :, None, :], cos, sin)
    k = jnp.concatenate([k_nope, jnp.broadcast_to(k_rope, (b, s, MLA_HEADS, MLA_ROPE))], axis=-1)
    o = causal_attention_blocks(q, k, v)
    return o.reshape(b, s, MLA_WIDTH)


def ab_mixer(h, w_in, conv_w, q_norm_g, kv_norm_g, w_uq, w_ukv, w_out, cos, sin):
    p = h @ w_in
    a_x, a_b, a_c, q_c, kv_c, k_r = jnp.split(p, AB_SPLITS, axis=-1)
    y_a = short_conv_mix(a_x, a_b, a_c, conv_w)
    y_b = mla_mix(q_c, kv_c, k_r, q_norm_g, kv_norm_g, w_uq, w_ukv, cos, sin)
    return jnp.concatenate([y_a, y_b], axis=-1) @ w_out


def chunkwise_retention(q, k, v):
    b, s, h, dk = q.shape
    dv = v.shape[-1]
    nc = s // RET_CHUNK
    lg = jnp.log1p(-jnp.exp2(-5.0 - jnp.arange(h, dtype=jnp.float32)))
    n = jnp.arange(RET_CHUNK, dtype=jnp.float32)
    diff = n[:, None] - n[None, :]
    inner_decay = jnp.where(diff >= 0, jnp.exp(lg[:, None, None] * jnp.maximum(diff, 0.0)), 0.0)
    q_decay = jnp.exp(lg[:, None] * (n + 1.0))
    k_decay = jnp.exp(lg[:, None] * (RET_CHUNK - 1.0 - n))
    chunk_decay = jnp.exp(lg * RET_CHUNK)

    def to_chunks(t):
        return t.astype(jnp.float32).reshape(b, nc, RET_CHUNK, h, -1).transpose(1, 0, 3, 2, 4)

    def step(state, inp):
        qc, kc, vc = inp
        sc = jnp.einsum('bhnd,bhmd->bhnm', qc, kc) * inner_decay
        o = jnp.einsum('bhnm,bhme->bhne', sc, vc) \
            + jnp.einsum('bhnd,bhde->bhne', qc * q_decay[..., None], state)
        state = chunk_decay[:, None, None] * state \
            + jnp.einsum('bhmd,bhme->bhde', kc * k_decay[..., None], vc)
        return state, o

    state0 = jnp.zeros((b, h, dk, dv), jnp.float32)
    _, o = lax.scan(step, state0, (to_chunks(q), to_chunks(k), to_chunks(v)))
    return o.transpose(1, 0, 3, 2, 4).reshape(b, s, h, dv)


def head_group_norm(o, g):
    mu = jnp.mean(o, axis=-1, keepdims=True)
    var = jnp.mean(jnp.square(o - mu), axis=-1, keepdims=True)
    y = (o - mu) * lax.rsqrt(var + LN_EPS)
    return y * g.astype(jnp.float32).reshape(RET_HEADS, RET_DV)


def retention_mixer(h, w_in, gn_g, w_out, cos, sin):
    b, s, _ = h.shape
    p = h @ w_in
    q, k, v, g = jnp.split(p, RET_SPLITS, axis=-1)
    q = apply_rope(q.reshape(b, s, RET_HEADS, RET_DK), cos, sin)
    k = apply_rope(k.reshape(b, s, RET_HEADS, RET_DK), cos, sin) * (RET_DK ** -0.5)
    v = v.reshape(b, s, RET_HEADS, RET_DV)
    o = head_group_norm(chunkwise_retention(q, k, v), gn_g).astype(h.dtype)
    y = jax.nn.silu(g) * o.reshape(b, s, RET_HEADS * RET_DV)
    return y @ w_out


def hier_moe(h, rg_w, rg_b, re_w, re_b, w_gate, w_up, w_down):
    b, s, d = h.shape
    hf = h.reshape(-1, d)
    g_prob = jax.nn.softmax((hf @ rg_w + rg_b).astype(jnp.float32), axis=-1)
    g_p, g_idx = lax.top_k(g_prob, 1)
    e_logits = (hf @ re_w + re_b).astype(jnp.float32).reshape(-1, N_GROUPS, EXPERTS_PER_GROUP)
    e_sel = jnp.take_along_axis(e_logits, g_idx[:, :, None], axis=1)[:, 0]
    e_prob = jax.nn.softmax(e_sel, axis=-1)
    e_p, e_idx = lax.top_k(e_prob, TOP_K)
    w = g_p * e_p / jnp.sum(e_p, axis=-1, keepdims=True)
    ids = g_idx * EXPERTS_PER_GROUP + e_idx
    combine = jnp.einsum('tk,tke->te', w, jax.nn.one_hot(ids, N_EXPERTS, dtype=jnp.float32))

    def expert_step(acc, params):
        wg, wu, wd, cw = params
        y = (jax.nn.silu(hf @ wg) * (hf @ wu)) @ wd
        return acc + cw[:, None] * y.astype(jnp.float32), None

    acc, _ = lax.scan(expert_step, jnp.zeros(hf.shape, jnp.float32),
                      (w_gate, w_up, w_down, combine.T))
    return acc.astype(h.dtype).reshape(b, s, d)


def setup_inputs(seed: int = 0) -> dict:
    key = jax.random.key(seed)
    ks = jax.random.split(key, 32)
    f32 = jnp.float32
    nrm = lambda k, shp, sc: jax.random.normal(k, shp, f32) * sc
    D = D_MODEL
    x = nrm(ks[0], (BATCH, SEQ, D), 1.0)
    c = nrm(ks[1], (BATCH, D), 1.0)
    positions = jnp.broadcast_to(jnp.arange(SEQ, dtype=jnp.int32)[None, :], (BATCH, SEQ))
    return {
        'x': x,
        'c': c,
        'positions': positions,
        'ada_mix_w': nrm(ks[2], (DEPTH, D, 3 * D), 0.2 * D ** -0.5),
        'ada_mix_b': nrm(ks[3], (DEPTH, 3 * D), 0.01),
        'ln_mix_g': 1.0 + nrm(ks[4], (DEPTH, D), 0.01),
        'ln_mix_b': nrm(ks[5], (DEPTH, D), 0.01),
        'ab_w_in': nrm(ks[6], (N_AB, D, AB_IN), D ** -0.5),
        'ab_conv_w': nrm(ks[7], (N_AB, CONV_K, CONV_WIDTH), CONV_K ** -0.5),
        'ab_q_norm_g': 1.0 + nrm(ks[8], (N_AB, MLA_Q_RANK), 0.01),
        'ab_kv_norm_g': 1.0 + nrm(ks[9], (N_AB, MLA_KV_RANK), 0.01),
        'ab_w_uq': nrm(ks[10], (N_AB, MLA_Q_RANK, MLA_HEADS * (MLA_NOPE + MLA_ROPE)), MLA_Q_RANK ** -0.5),
        'ab_w_ukv': nrm(ks[11], (N_AB, MLA_KV_RANK, MLA_HEADS * (MLA_NOPE + MLA_V)), MLA_KV_RANK ** -0.5),
        'ab_w_out': nrm(ks[12], (N_AB, AB_OUT, D), DEEPNORM_BETA * AB_OUT ** -0.5),
        'ret_w_in': nrm(ks[13], (N_RET, D, RET_IN), D ** -0.5),
        'ret_gn_g': 1.0 + nrm(ks[14], (N_RET, RET_HEADS * RET_DV), 0.01),
        'ret_w_out': nrm(ks[15], (N_RET, RET_HEADS * RET_DV, D), DEEPNORM_BETA * (RET_HEADS * RET_DV) ** -0.5),
        'ada_ffn_w': nrm(ks[16], (DEPTH, D, 3 * D), 0.2 * D ** -0.5),
        'ada_ffn_b': nrm(ks[17], (DEPTH, 3 * D), 0.01),
        'ln_ffn_g': 1.0 + nrm(ks[18], (DEPTH, D), 0.01),
        'ln_ffn_b': nrm(ks[19], (DEPTH, D), 0.01),
        'moe_rg_w': nrm(ks[20], (DEPTH, D, N_GROUPS), D ** -0.5),
        'moe_rg_b': nrm(ks[21], (DEPTH, N_GROUPS), 0.01),
        'moe_re_w': nrm(ks[22], (DEPTH, D, N_EXPERTS), D ** -0.5),
        'moe_re_b': nrm(ks[23], (DEPTH, N_EXPERTS), 0.01),
        'moe_w_gate': nrm(ks[24], (DEPTH, N_EXPERTS, D, D_EXPERT), D ** -0.5),
        'moe_w_up': nrm(ks[25], (DEPTH, N_EXPERTS, D, D_EXPERT), D ** -0.5),
        'moe_w_down': nrm(ks[26], (DEPTH, N_EXPERTS, D_EXPERT, D), DEEPNORM_BETA * D_EXPERT ** -0.5),
    }


def reference(x, c, positions, ada_mix_w, ada_mix_b, ln_mix_g, ln_mix_b,
              ab_w_in, ab_conv_w, ab_q_norm_g, ab_kv_norm_g, ab_w_uq, ab_w_ukv, ab_w_out,
              ret_w_in, ret_gn_g, ret_w_out,
              ada_ffn_w, ada_ffn_b, ln_ffn_g, ln_ffn_b,
              moe_rg_w, moe_rg_b, moe_re_w, moe_re_b, moe_w_gate, moe_w_up, moe_w_down):
    cos_m, sin_m = rope_tables(positions, MLA_ROPE)
    cos_r, sin_r = rope_tables(positions, RET_DK)
    for l in range(DEPTH):
        shift, scale, gate = adaln(c, ada_mix_w[l], ada_mix_b[l])
        h = modulate(x, shift, scale)
        if l % 2 == 0:
            i = l // 2
            y = ab_mixer(h, ab_w_in[i], ab_conv_w[i], ab_q_norm_g[i], ab_kv_norm_g[i],
                         ab_w_uq[i], ab_w_ukv[i], ab_w_out[i], cos_m, sin_m)
        else:
            i = l // 2
            y = retention_mixer(h, ret_w_in[i], ret_gn_g[i], ret_w_out[i], cos_r, sin_r)
        x = layer_norm(DEEPNORM_ALPHA * x + (1.0 + gate[:, None, :]) * y, ln_mix_g[l], ln_mix_b[l])
        shift, scale, gate = adaln(c, ada_ffn_w[l], ada_ffn_b[l])
        h = modulate(x, shift, scale)
        y = hier_moe(h, moe_rg_w[l], moe_rg_b[l], moe_re_w[l], moe_re_b[l],
                     moe_w_gate[l], moe_w_up[l], moe_w_down[l])
        x = layer_norm(DEEPNORM_ALPHA * x + (1.0 + gate[:, None, :]) * y, ln_ffn_g[l], ln_ffn_b[l])
    return x
```

```python
import functools

import numpy as np
import jax
import jax.numpy as jnp
from jax import lax
from jax.experimental import pallas as pl
from jax.experimental.pallas import tpu as pltpu

F32 = jnp.float32
BF16 = jnp.bfloat16
I32 = jnp.int32
HIGHEST = lax.Precision.HIGHEST

D_MODEL = 1024
DEPTH = 4
CONV_WIDTH = 512
CONV_K = 3
MLA_HEADS = 8
MLA_NOPE = 64
MLA_ROPE = 32
MLA_V = 64
MLA_Q_RANK = 256
MLA_KV_RANK = 128
ROPE_BASE = 10000.0
RET_HEADS = 8
RET_DK = 128
RET_DV = 256
RET_CHUNK = 128
N_GROUPS = 4
EXPERTS_PER_GROUP = 8
N_EXPERTS = 32
D_EXPERT = 256
LN_EPS = 1e-5
RMS_EPS = 1e-6
DEEPNORM_ALPHA = (2 * DEPTH) ** 0.25

LANES = 128
SUBLANES = 8
VMEM_LIMIT_BYTES = 56 * 2 ** 20
HEAD_PAD = LANES
TM_PROJ = 256
TM_ROUTE = 256
TQ_ATTN = 512
TM_MOE = 256
TM_EXPERT = 256
ROUTER_ROWS = 40
NEG_BIG = -1e30


def _cparams(n_axes):
    return pltpu.CompilerParams(dimension_semantics=("arbitrary",) * n_axes,
                                vmem_limit_bytes=VMEM_LIMIT_BYTES)


def _silu(x):
    return x * (1.0 / (1.0 + jnp.exp(-x)))


def _adaln_kernel(c_ref, w_ref, b_ref, o_ref):
    c = c_ref[...]
    o_ref[0] = jnp.dot(_silu(c), w_ref[0], precision=HIGHEST, preferred_element_type=F32) + b_ref[0]


def _adaln(c, w, b):
    n_l, d, n3 = w.shape
    bsz = c.shape[0]
    tn = 1024
    return pl.pallas_call(
        _adaln_kernel,
        grid=(n_l, n3 // tn),
        in_specs=[pl.BlockSpec((bsz, d), lambda l, j: (0, 0)),
                  pl.BlockSpec((1, d, tn), lambda l, j: (l, 0, j)),
                  pl.BlockSpec((1, 1, tn), lambda l, j: (l, 0, j))],
        out_specs=pl.BlockSpec((1, bsz, tn), lambda l, j: (l, 0, j)),
        out_shape=jax.ShapeDtypeStruct((n_l, bsz, n3), F32),
        compiler_params=_cparams(2),
        name="adaln",
    )(c, w, b.reshape(n_l, 1, n3))


AB_COLS = 3 * CONV_WIDTH + MLA_Q_RANK + MLA_KV_RANK + 2 * LANES
AB_QC = 3 * CONV_WIDTH
AB_KVC = AB_QC + MLA_Q_RANK
AB_KR = AB_KVC + MLA_KV_RANK
AB_KRROT = AB_KR + LANES
N_HP = MLA_HEADS * HEAD_PAD


def _ab_in_kernel(x_ref, sh_ref, sc_ref, w_ref, cw_ref, qg_ref, kg_ref, wqa_ref, wqb_ref,
                  wk_ref, wv_ref, c_ref, s_ref, ya_ref, q_ref, k_ref, v_ref, carry_ref,
                  *, tm, tiles_per_seq):
    i = pl.program_id(0)
    h = (x_ref[...] * sc_ref[0] + sh_ref[0]).astype(BF16)
    p = jnp.dot(h, w_ref[...], preferred_element_type=F32)

    u = p[:, 2 * CONV_WIDTH:3 * CONV_WIDTH] * p[:, 0:CONV_WIDTH]
    first = (i % tiles_per_seq) == 0
    prev = jnp.where(first, 0.0, carry_ref[...])
    row = lax.broadcasted_iota(I32, u.shape, 0)
    p7 = prev[SUBLANES - 1:SUBLANES, :]
    p6 = prev[SUBLANES - 2:SUBLANES - 1, :]
    u1 = jnp.where(row == 0, p7, pltpu.roll(u, 1, 0))
    u2 = jnp.where(row == 0, p6, jnp.where(row == 1, p7, pltpu.roll(u, 2, 0)))
    carry_ref[...] = u[tm - SUBLANES:tm, :]
    cw = cw_ref[...]
    y = cw[0:1, :] * u2 + cw[1:2, :] * u1 + cw[2:3, :] * u
    ya_ref[...] = (p[:, CONV_WIDTH:2 * CONV_WIDTH] * y).astype(BF16)

    cos = c_ref[...]
    sin = s_ref[...]
    cos_h = jnp.concatenate([cos] * MLA_HEADS, axis=1)
    sin_h = jnp.concatenate([sin] * MLA_HEADS, axis=1)

    q_c = p[:, AB_QC:AB_KVC]
    qn = q_c * lax.rsqrt(jnp.mean(q_c * q_c, axis=1, keepdims=True) + RMS_EPS) * qg_ref[...]
    qn = qn.astype(BF16)
    qa = jnp.dot(qn, wqa_ref[...], preferred_element_type=F32)
    qb = jnp.dot(qn, wqb_ref[...], preferred_element_type=F32)
    q_ref[...] = (qa * cos_h + qb * sin_h).astype(BF16)

    kv_c = p[:, AB_KVC:AB_KR]
    kvn = kv_c * lax.rsqrt(jnp.mean(kv_c * kv_c, axis=1, keepdims=True) + RMS_EPS) * kg_ref[...]
    kvn = kvn.astype(BF16)
    kr = p[:, AB_KR:AB_KR + LANES] * cos + p[:, AB_KRROT:AB_KRROT + LANES] * sin
    kn = jnp.dot(kvn, wk_ref[...], preferred_element_type=F32)
    k_ref[...] = (kn + jnp.concatenate([kr] * MLA_HEADS, axis=1)).astype(BF16)
    v_ref[...] = jnp.dot(kvn, wv_ref[...], preferred_element_type=F32).astype(BF16)


def _ab_in(x2, shift, scale1p, w_ext, conv_w, qg, kg, wqa, wqb, wk, wv, cos128, sin128, seq):
    t, d = x2.shape
    tm = TM_PROJ
    tps = seq // tm
    full = lambda shp: pl.BlockSpec(shp, lambda i: (0,) * len(shp))
    mod = pl.BlockSpec((1, 1, d), lambda i: (i // tps, 0, 0))
    tok = lambda n: pl.BlockSpec((tm, n), lambda i: (i, 0))
    return pl.pallas_call(
        functools.partial(_ab_in_kernel, tm=tm, tiles_per_seq=tps),
        grid=(t // tm,),
        in_specs=[tok(d), mod, mod, full((d, AB_COLS)), full((CONV_K, CONV_WIDTH)),
                  full((1, MLA_Q_RANK)), full((1, MLA_KV_RANK)),
                  full((MLA_Q_RANK, N_HP)), full((MLA_Q_RANK, N_HP)),
                  full((MLA_KV_RANK, N_HP)), full((MLA_KV_RANK, N_HP)),
                  tok(LANES), tok(LANES)],
        out_specs=[tok(CONV_WIDTH), tok(N_HP), tok(N_HP), tok(N_HP)],
        out_shape=[jax.ShapeDtypeStruct((t, CONV_WIDTH), BF16)] + [jax.ShapeDtypeStruct((t, N_HP), BF16)] * 3,
        scratch_shapes=[pltpu.VMEM((SUBLANES, CONV_WIDTH), F32)],
        compiler_params=_cparams(1),
        name="ab_in",
    )(x2, shift, scale1p, w_ext, conv_w, qg, kg, wqa, wqb, wk, wv, cos128, sin128)


def _attn_kernel(q_ref, k_ref, v_ref, o_ref, *, tq):
    i = pl.program_id(1)
    nt = (((1,), (1,)), ((), ()))
    rows = lax.broadcasted_iota(I32, (tq, tq), 0)
    cols = lax.broadcasted_iota(I32, (tq, tq), 1)
    causal = cols <= rows

    def step(q, kb, vb, carry, mask):
        m, l, acc = carry
        s = lax.dot_general(q, kb, nt, preferred_element_type=F32)
        if mask:
            s = jnp.where(causal, s, -jnp.inf)
        m_new = jnp.maximum(m, jnp.max(s, axis=1, keepdims=True))
        a = jnp.exp(m - m_new)
        p = jnp.exp(s - m_new)
        l = a * l + jnp.sum(p, axis=1, keepdims=True)
        acc = a * acc + jnp.dot(p.astype(BF16), vb, preferred_element_type=F32)
        return m_new, l, acc

    for h in range(MLA_HEADS):
        lo = h * HEAD_PAD
        q = q_ref[:, lo:lo + HEAD_PAD]

        def body(j, carry, q=q, lo=lo):
            r0 = pl.multiple_of(j * tq, tq)
            return step(q, k_ref[pl.ds(r0, tq), lo:lo + HEAD_PAD], v_ref[pl.ds(r0, tq), lo:lo + HEAD_PAD],
                        carry, False)

        init = (jnp.full((tq, 1), -jnp.inf, F32), jnp.zeros((tq, 1), F32), jnp.zeros((tq, HEAD_PAD), F32))
        carry = lax.fori_loop(0, i, body, init)
        r0 = pl.multiple_of(i * tq, tq)
        m, l, acc = step(q, k_ref[pl.ds(r0, tq), lo:lo + HEAD_PAD], v_ref[pl.ds(r0, tq), lo:lo + HEAD_PAD],
                         carry, True)
        o_ref[:, lo:lo + HEAD_PAD] = (acc / l).astype(BF16)


def _attn(q, k, v, bsz, seq):
    tq = TQ_ATTN
    nq = seq // tq
    return pl.pallas_call(
        functools.partial(_attn_kernel, tq=tq),
        grid=(bsz, nq),
        in_specs=[pl.BlockSpec((tq, N_HP), lambda b, i: (b * nq + i, 0)),
                  pl.BlockSpec((seq, N_HP), lambda b, i: (b, 0)),
                  pl.BlockSpec((seq, N_HP), lambda b, i: (b, 0))],
        out_specs=pl.BlockSpec((tq, N_HP), lambda b, i: (b * nq + i, 0)),
        out_shape=jax.ShapeDtypeStruct(q.shape, BF16),
        compiler_params=_cparams(2),
        name="attn",
    )(q, k, v)


RET_QK = RET_HEADS * RET_DK
RET_V = RET_HEADS * RET_DV
RET_COLS = 4 * RET_QK + 2 * RET_V


def _ret_in_kernel(x_ref, sh_ref, sc_ref, w_ref, c_ref, s_ref, q_ref, k_ref, v_ref, g_ref):
    h = (x_ref[...] * sc_ref[0] + sh_ref[0]).astype(BF16)
    cos_h = jnp.concatenate([c_ref[...]] * RET_HEADS, axis=1)
    sin_h = jnp.concatenate([s_ref[...]] * RET_HEADS, axis=1)
    pq = jnp.dot(h, w_ref[:, 0:2 * RET_QK], preferred_element_type=F32)
    q_ref[...] = (pq[:, :RET_QK] * cos_h + pq[:, RET_QK:] * sin_h).astype(BF16)
    pk = jnp.dot(h, w_ref[:, 2 * RET_QK:4 * RET_QK], preferred_element_type=F32)
    k_ref[...] = ((pk[:, :RET_QK] * cos_h + pk[:, RET_QK:] * sin_h) * (RET_DK ** -0.5)).astype(BF16)
    v_ref[...] = jnp.dot(h, w_ref[:, 4 * RET_QK:4 * RET_QK + RET_V], preferred_element_type=F32).astype(BF16)
    g = jnp.dot(h, w_ref[:, 4 * RET_QK + RET_V:RET_COLS], preferred_element_type=F32)
    g_ref[...] = _silu(g).astype(BF16)


def _ret_in(x2, shift, scale1p, w_ext, cos128, sin128, seq):
    t, d = x2.shape
    tm = TM_PROJ
    tps = seq // tm
    mod = pl.BlockSpec((1, 1, d), lambda i: (i // tps, 0, 0))
    tok = lambda n: pl.BlockSpec((tm, n), lambda i: (i, 0))
    return pl.pallas_call(
        _ret_in_kernel,
        grid=(t // tm,),
        in_specs=[tok(d), mod, mod,
                  pl.BlockSpec((d, RET_COLS), lambda i: (0, 0), pipeline_mode=pl.Buffered(1)),
                  tok(LANES), tok(LANES)],
        out_specs=[tok(RET_QK), tok(RET_QK), tok(RET_V), tok(RET_V)],
        out_shape=[jax.ShapeDtypeStruct((t, RET_QK), BF16), jax.ShapeDtypeStruct((t, RET_QK), BF16),
                   jax.ShapeDtypeStruct((t, RET_V), BF16), jax.ShapeDtypeStruct((t, RET_V), BF16)],
        compiler_params=_cparams(1),
        name="ret_in",
    )(x2, shift, scale1p, w_ext, cos128, sin128)


def _retention_kernel(q_ref, k_ref, v_ref, g_ref, dm_ref, qd_ref, kd_ref, gn_ref, o_ref, st_ref, *, cdec):
    c = pl.program_id(1)

    @pl.when(c == 0)
    def _():
        st_ref[...] = jnp.zeros_like(st_ref)

    nt = (((1,), (1,)), ((), ()))
    for h in range(RET_HEADS):
        qh = q_ref[:, h * RET_DK:(h + 1) * RET_DK]
        kh = k_ref[:, h * RET_DK:(h + 1) * RET_DK]
        vh = v_ref[:, h * RET_DV:(h + 1) * RET_DV]
        st = st_ref[h]
        sc = lax.dot_general(qh, kh, nt, preferred_element_type=F32) * dm_ref[h]
        o = jnp.dot(sc.astype(BF16), vh, preferred_element_type=F32)
        o = o + qd_ref[h] * jnp.dot(qh, st.astype(BF16), preferred_element_type=F32)
        kt = (kh.astype(F32) * kd_ref[h]).T.astype(BF16)
        st_ref[h] = cdec[h] * st + jnp.dot(kt, vh, preferred_element_type=F32)
        mu = jnp.mean(o, axis=1, keepdims=True)
        oc = o - mu
        var = jnp.mean(oc * oc, axis=1, keepdims=True)
        y = oc * lax.rsqrt(var + LN_EPS) * gn_ref[:, h * RET_DV:(h + 1) * RET_DV]
        o_ref[:, h * RET_DV:(h + 1) * RET_DV] = (g_ref[:, h * RET_DV:(h + 1) * RET_DV].astype(F32) * y).astype(BF16)


def _retention_tables():
    hh = np.arange(RET_HEADS, dtype=np.float64)
    lg = np.log1p(-np.exp2(-5.0 - hh))
    n = np.arange(RET_CHUNK, dtype=np.float64)
    diff = n[:, None] - n[None, :]
    dm = np.where(diff >= 0, np.exp(lg[:, None, None] * np.maximum(diff, 0.0)), 0.0)
    qd = np.exp(lg[:, None] * (n + 1.0))[:, :, None]
    kd = np.exp(lg[:, None] * (RET_CHUNK - 1.0 - n))[:, :, None]
    cd = np.exp(lg * RET_CHUNK)
    return (jnp.asarray(dm, F32), jnp.asarray(qd, F32), jnp.asarray(kd, F32),
            tuple(float(np.float32(v)) for v in cd))


def _retention(q, k, v, gs, gn_g, bsz, seq):
    t = q.shape[0]
    nc = seq // RET_CHUNK
    dm, qd, kd, cdec = _retention_tables()
    tok = lambda n: pl.BlockSpec((RET_CHUNK, n), lambda b, c: (b * nc + c, 0))
    full = lambda shp: pl.BlockSpec(shp, lambda b, c: (0,) * len(shp))
    return pl.pallas_call(
        functools.partial(_retention_kernel, cdec=cdec),
        grid=(bsz, nc),
        in_specs=[tok(RET_QK), tok(RET_QK), tok(RET_V), tok(RET_V),
                  full((RET_HEADS, RET_CHUNK, RET_CHUNK)), full((RET_HEADS, RET_CHUNK, 1)),
                  full((RET_HEADS, RET_CHUNK, 1)), full((1, RET_V))],
        out_specs=tok(RET_V),
        out_shape=jax.ShapeDtypeStruct((t, RET_V), BF16),
        scratch_shapes=[pltpu.VMEM((RET_HEADS, RET_DK, RET_DV), F32)],
        compiler_params=_cparams(2),
        name="retention",
    )(q, k, v, gs, dm, qd, kd, gn_g)


def _layer_norm(z, g, b):
    mu = jnp.mean(z, axis=1, keepdims=True)
    zc = z - mu
    var = jnp.mean(zc * zc, axis=1, keepdims=True)
    return zc * lax.rsqrt(var + LN_EPS) * g + b


def _out_ln_route_kernel(*refs, n_y, tm):
    y_refs = refs[:n_y]
    w_refs = refs[n_y:2 * n_y]
    (x_ref, g1_ref, lng_ref, lnb_ref, sh2_ref, sc2_ref, wr_ref, br_ref,
     xo_ref, ids_ref, rank_ref, info_ref, cnt_ref, cnt_sc) = refs[2 * n_y:]
    i = pl.program_id(0)

    y = jnp.dot(y_refs[0][...], w_refs[0][...], preferred_element_type=F32)
    for yr, wr in zip(y_refs[1:], w_refs[1:]):
        y = y + jnp.dot(yr[...], wr[...], preferred_element_type=F32)
    xn = _layer_norm(DEEPNORM_ALPHA * x_ref[...] + g1_ref[0] * y, lng_ref[...], lnb_ref[...])
    xo_ref[...] = xn

    h2 = xn * sc2_ref[0] + sh2_ref[0]
    lg = lax.dot_general(wr_ref[...], h2, (((1,), (1,)), ((), ())), precision=HIGHEST,
                         preferred_element_type=F32) + br_ref[...]
    r8 = lax.broadcasted_iota(I32, (SUBLANES, tm), 0)
    gl = lg[0:SUBLANES]
    ge = jnp.exp(gl - jnp.max(gl, axis=0, keepdims=True))
    gprob = ge / jnp.sum(ge, axis=0, keepdims=True)
    g_p = jnp.max(gprob, axis=0, keepdims=True)
    g_idx = jnp.min(jnp.where(gprob == g_p, r8, SUBLANES), axis=0, keepdims=True)
    el = lg[SUBLANES:ROUTER_ROWS]
    e_sel = el[3 * SUBLANES:4 * SUBLANES]
    for g in (2, 1, 0):
        e_sel = jnp.where(g_idx == g, el[g * SUBLANES:(g + 1) * SUBLANES], e_sel)
    ee = jnp.exp(e_sel - jnp.max(e_sel, axis=0, keepdims=True))
    eprob = ee / jnp.sum(ee, axis=0, keepdims=True)
    p1 = jnp.max(eprob, axis=0, keepdims=True)
    i1 = jnp.min(jnp.where(eprob == p1, r8, SUBLANES), axis=0, keepdims=True)
    rest = jnp.where(r8 == i1, -1.0, eprob)
    p2 = jnp.max(rest, axis=0, keepdims=True)
    i2 = jnp.min(jnp.where(rest == p2, r8, SUBLANES), axis=0, keepdims=True)
    den = p1 + p2
    w1 = g_p * p1 / den
    w2 = g_p * p2 / den
    id1 = g_idx * EXPERTS_PER_GROUP + i1
    id2 = g_idx * EXPERTS_PER_GROUP + i2

    @pl.when(i == 0)
    def _():
        cnt_sc[...] = jnp.zeros_like(cnt_sc)

    r32 = lax.broadcasted_iota(I32, (N_EXPERTS, tm), 0)
    is1 = r32 == id1
    is2 = r32 == id2
    onehot = jnp.where(is1 | is2, 1.0, 0.0)
    before = (lax.broadcasted_iota(I32, (tm, tm), 0) < lax.broadcasted_iota(I32, (tm, tm), 1))
    prefix = jnp.dot(onehot.astype(BF16), jnp.where(before, 1.0, 0.0).astype(BF16), preferred_element_type=F32)
    tot = prefix + cnt_sc[:, 0:1]
    rank1 = jnp.sum(jnp.where(is1, tot, 0.0), axis=0, keepdims=True)
    rank2 = jnp.sum(jnp.where(is2, tot, 0.0), axis=0, keepdims=True)
    cnt_sc[...] = cnt_sc[...] + jnp.sum(onehot, axis=1, keepdims=True)
    cnt_ref[...] = cnt_sc[...]

    ids_ref[...] = jnp.concatenate([id1, id2], axis=0)
    rank_ref[...] = jnp.concatenate([rank1, rank2], axis=0).astype(I32)
    packed = jnp.concatenate([w1, w2, jnp.zeros((LANES - 2, tm), F32)], axis=0)
    info_ref[...] = packed.T


def _out_ln_route(ys, ws, x2, gate1p, ln_g, ln_b, shift2, scale2_1p, wr_t, br, seq):
    t, d = x2.shape
    tm = TM_ROUTE
    tps = seq // tm
    n_y = len(ys)
    full = lambda shp: pl.BlockSpec(shp, lambda i: (0,) * len(shp))
    mod = pl.BlockSpec((1, 1, d), lambda i: (i // tps, 0, 0))
    tok = lambda n: pl.BlockSpec((tm, n), lambda i: (i, 0))
    lane = pl.BlockSpec((2, tm), lambda i: (0, i))
    in_specs = ([tok(y.shape[1]) for y in ys] + [full(w.shape) for w in ws]
                + [tok(d), mod, full((1, d)), full((1, d)), mod, mod,
                   full((ROUTER_ROWS, d)), full((ROUTER_ROWS, 1))])
    return pl.pallas_call(
        functools.partial(_out_ln_route_kernel, n_y=n_y, tm=tm),
        grid=(t // tm,),
        in_specs=in_specs,
        out_specs=[tok(d), lane, lane, tok(LANES), full((N_EXPERTS, LANES))],
        out_shape=[jax.ShapeDtypeStruct((t, d), F32), jax.ShapeDtypeStruct((2, t), I32),
                   jax.ShapeDtypeStruct((2, t), I32), jax.ShapeDtypeStruct((t, LANES), F32),
                   jax.ShapeDtypeStruct((N_EXPERTS, LANES), F32)],
        scratch_shapes=[pltpu.VMEM((N_EXPERTS, LANES), F32)],
        compiler_params=_cparams(1),
        name="out_ln_route",
    )(*ys, *ws, x2, gate1p, ln_g, ln_b, shift2, scale2_1p, wr_t, br)


ROW_TILE = D_MODEL // LANES


def _rows_to_tiles(dst_ref, base, val, n):
    for s in range(ROW_TILE):
        dst_ref[pl.ds(base + s, n, stride=ROW_TILE), :] = val[:, s * LANES:(s + 1) * LANES]


def _tiles_to_rows(src_ref, base, n):
    return jnp.concatenate([src_ref[pl.ds(base + s, n, stride=ROW_TILE), :] for s in range(ROW_TILE)], axis=1)


def _tile_rows(ref, r, n=1):
    return ref.at[pl.ds(pl.multiple_of(r * ROW_TILE, ROW_TILE), n * ROW_TILE), :]


def _dispatch_kernel(off_ref, psz_ref, ids_ref, rank_ref, x_ref, sh_ref, sc_ref,
                     xs_ref, pos_ref, hbuf, zbuf, sem, zsem, *, tm, n_steps, n_tiles):
    i = pl.program_id(0)
    slot = i % 2

    def zero_copy(e):
        return pltpu.make_async_copy(zbuf, _tile_rows(xs_ref, off_ref[e] + psz_ref[e] - TM_EXPERT, TM_EXPERT), zsem)

    @pl.when(i == 0)
    def _():
        zbuf[...] = jnp.zeros_like(zbuf)
        for e in range(N_EXPERTS):
            @pl.when(psz_ref[e] > 0)
            def _(e=e):
                zero_copy(e).start()
        for e in range(N_EXPERTS):
            @pl.when(psz_ref[e] > 0)
            def _(e=e):
                zero_copy(e).wait()

        used = (off_ref[N_EXPERTS - 1] + psz_ref[N_EXPERTS - 1]) // TM_EXPERT

        def tail_copy(j):
            return pltpu.make_async_copy(zbuf, _tile_rows(xs_ref, j * TM_EXPERT, TM_EXPERT), zsem)

        def tail_start(j, carry):
            tail_copy(j).start()
            return carry

        def tail_wait(j, carry):
            tail_copy(j).wait()
            return carry

        lax.fori_loop(used, n_tiles, tail_start, 0)
        lax.fori_loop(used, n_tiles, tail_wait, 0)

    _rows_to_tiles(hbuf, slot * tm * ROW_TILE, x_ref[...] * sc_ref[0] + sh_ref[0], tm)

    def issue(t, carry):
        for k in range(2):
            p = off_ref[ids_ref[k, t]] + rank_ref[k, t]
            pos_ref[k, t] = p
            pltpu.make_async_copy(_tile_rows(hbuf, slot * tm + t), _tile_rows(xs_ref, p), sem.at[slot]).start()
        return carry

    lax.fori_loop(0, tm, issue, 0, unroll=8)

    def drain(s):
        for _ in range(2):
            pltpu.make_async_copy(_tile_rows(hbuf, s * tm, tm), _tile_rows(xs_ref, 0, tm), sem.at[s]).wait()

    @pl.when(i > 0)
    def _():
        drain(1 - slot)

    @pl.when(i == n_steps - 1)
    def _():
        drain(slot)


def _dispatch(off, psz, ids, rank, x2, shift2, scale2_1p, n_rows, seq):
    t, d = x2.shape
    tm = TM_MOE
    tps = seq // tm
    n_steps = t // tm
    smem = pl.BlockSpec((2, tm), lambda i, *_: (0, i), memory_space=pltpu.SMEM)
    mod = pl.BlockSpec((1, 1, d), lambda i, *_: (i // tps, 0, 0))
    grid_spec = pltpu.PrefetchScalarGridSpec(
        num_scalar_prefetch=2,
        grid=(n_steps,),
        in_specs=[smem, smem, pl.BlockSpec((tm, d), lambda i, *_: (i, 0)), mod, mod],
        out_specs=[pl.BlockSpec(memory_space=pl.ANY), smem],
        scratch_shapes=[pltpu.VMEM((2 * tm * ROW_TILE, LANES), F32), pltpu.VMEM((TM_EXPERT * ROW_TILE, LANES), F32),
                        pltpu.SemaphoreType.DMA((2,)), pltpu.SemaphoreType.DMA],
    )
    return pl.pallas_call(
        functools.partial(_dispatch_kernel, tm=tm, n_steps=n_steps, n_tiles=n_rows // TM_EXPERT),
        grid_spec=grid_spec,
        out_shape=[jax.ShapeDtypeStruct((n_rows * ROW_TILE, LANES), F32), jax.ShapeDtypeStruct((2, t), I32)],
        compiler_params=_cparams(1),
        name="moe_dispatch",
    )(off, psz, ids, rank, x2, shift2, scale2_1p)


def _experts_kernel(te_ref, nu_ref, xs_ref, wg_ref, wu_ref, wd_ref, ys_ref):
    @pl.when(pl.program_id(0) >= nu_ref[0])
    def _():
        ys_ref[...] = jnp.zeros_like(ys_ref)

    @pl.when(pl.program_id(0) < nu_ref[0])
    def _():
        x = _tiles_to_rows(xs_ref, 0, TM_EXPERT).astype(BF16)
        a = _silu(jnp.dot(x, wg_ref[0], preferred_element_type=F32)) * jnp.dot(x, wu_ref[0], preferred_element_type=F32)
        _rows_to_tiles(ys_ref, 0, jnp.dot(a.astype(BF16), wd_ref[0], preferred_element_type=F32), TM_EXPERT)


def _experts(te, nused, xs, wg, wu, wd):
    d = D_MODEL
    n_tiles = xs.shape[0] // (ROW_TILE * TM_EXPERT)
    row = lambda i, te_ref, nu_ref: (jnp.minimum(i, nu_ref[0] - 1), 0)
    grid_spec = pltpu.PrefetchScalarGridSpec(
        num_scalar_prefetch=2,
        grid=(n_tiles,),
        in_specs=[pl.BlockSpec((TM_EXPERT * ROW_TILE, LANES), row),
                  pl.BlockSpec((1, d, D_EXPERT), lambda i, te_ref, nu_ref: (te_ref[i], 0, 0)),
                  pl.BlockSpec((1, d, D_EXPERT), lambda i, te_ref, nu_ref: (te_ref[i], 0, 0)),
                  pl.BlockSpec((1, D_EXPERT, d), lambda i, te_ref, nu_ref: (te_ref[i], 0, 0))],
        out_specs=pl.BlockSpec((TM_EXPERT * ROW_TILE, LANES), lambda i, te_ref, nu_ref: (i, 0)),
    )
    return pl.pallas_call(
        _experts_kernel,
        grid_spec=grid_spec,
        out_shape=jax.ShapeDtypeStruct(xs.shape, F32),
        compiler_params=_cparams(1),
        name="moe_experts",
    )(te, nused, xs, wg, wu, wd)


def _combine_ln_kernel(pos_ref, info_ref, x_ref, g1_ref, lng_ref, lnb_ref, ys_ref, xo_ref, ybuf, sem, *, tm):
    def issue(t, carry):
        for k in range(2):
            pltpu.make_async_copy(_tile_rows(ys_ref, pos_ref[k, t]), _tile_rows(ybuf, k * tm + t), sem).start()
        return carry

    lax.fori_loop(0, tm, issue, 0, unroll=8)
    for k in range(2):
        pltpu.make_async_copy(_tile_rows(ys_ref, 0, tm), _tile_rows(ybuf, k * tm, tm), sem).wait()

    info = info_ref[...]
    y = (info[:, 0:1] * _tiles_to_rows(ybuf, 0, tm)
         + info[:, 1:2] * _tiles_to_rows(ybuf, tm * ROW_TILE, tm))
    xo_ref[...] = _layer_norm(DEEPNORM_ALPHA * x_ref[...] + g1_ref[0] * y, lng_ref[...], lnb_ref[...])


def _combine_ln(pos, info, x2, gate1p, ln_g, ln_b, ys, seq):
    t, d = x2.shape
    tm = TM_MOE
    tps = seq // tm
    full = lambda shp: pl.BlockSpec(shp, lambda i: (0,) * len(shp))
    return pl.pallas_call(
        functools.partial(_combine_ln_kernel, tm=tm),
        grid=(t // tm,),
        in_specs=[pl.BlockSpec((2, tm), lambda i: (0, i), memory_space=pltpu.SMEM),
                  pl.BlockSpec((tm, LANES), lambda i: (i, 0)),
                  pl.BlockSpec((tm, d), lambda i: (i, 0)),
                  pl.BlockSpec((1, 1, d), lambda i: (i // tps, 0, 0)),
                  full((1, d)), full((1, d)),
                  pl.BlockSpec(memory_space=pl.ANY)],
        out_specs=pl.BlockSpec((tm, d), lambda i: (i, 0)),
        out_shape=jax.ShapeDtypeStruct((t, d), F32),
        scratch_shapes=[pltpu.VMEM((2 * tm * ROW_TILE, LANES), F32), pltpu.SemaphoreType.DMA],
        compiler_params=_cparams(1),
        name="moe_combine_ln",
    )(pos, info, x2, gate1p, ln_g, ln_b, ys)


def _rot_half(w, half):
    return jnp.concatenate([-w[..., half:], w[..., :half]], axis=-1)


def _prep_ab(w_in, w_uq, w_ukv, w_out):
    d = w_in.shape[0]
    half = MLA_ROPE // 2
    kr = w_in[:, AB_KR:AB_KR + MLA_ROPE]
    z = lambda n: jnp.zeros((d, n), F32)
    w_ext = jnp.concatenate([w_in[:, :AB_KR], z(MLA_NOPE), kr, z(HEAD_PAD - MLA_NOPE - MLA_ROPE),
                             z(MLA_NOPE), _rot_half(kr, half), z(HEAD_PAD - MLA_NOPE - MLA_ROPE)], axis=1)
    scale = (MLA_NOPE + MLA_ROPE) ** -0.5
    wq = w_uq.reshape(MLA_Q_RANK, MLA_HEADS, MLA_NOPE + MLA_ROPE) * scale
    nope, rope = wq[..., :MLA_NOPE], wq[..., MLA_NOPE:]
    zq = lambda n: jnp.zeros((MLA_Q_RANK, MLA_HEADS, n), F32)
    pad = HEAD_PAD - MLA_NOPE - MLA_ROPE
    wqa = jnp.concatenate([nope, rope, zq(pad)], axis=-1).reshape(MLA_Q_RANK, N_HP)
    wqb = jnp.concatenate([zq(MLA_NOPE), _rot_half(rope, half), zq(pad)], axis=-1).reshape(MLA_Q_RANK, N_HP)
    wkv = w_ukv.reshape(MLA_KV_RANK, MLA_HEADS, MLA_NOPE + MLA_V)
    zk = jnp.zeros((MLA_KV_RANK, MLA_HEADS, HEAD_PAD - MLA_NOPE), F32)
    wk = jnp.concatenate([wkv[..., :MLA_NOPE], zk], axis=-1).reshape(MLA_KV_RANK, N_HP)
    wv = jnp.concatenate([wkv[..., MLA_NOPE:], zk], axis=-1).reshape(MLA_KV_RANK, N_HP)
    wo_b = w_out[CONV_WIDTH:].reshape(MLA_HEADS, MLA_V, d)
    wo_b = jnp.concatenate([wo_b, jnp.zeros((MLA_HEADS, HEAD_PAD - MLA_V, d), F32)], axis=1).reshape(N_HP, d)
    bf = lambda a: a.astype(BF16)
    return bf(w_ext), bf(wqa), bf(wqb), bf(wk), bf(wv), bf(w_out[:CONV_WIDTH]), bf(wo_b)


def _prep_ret(w_in):
    d = w_in.shape[0]
    wq = w_in[:, :RET_QK].reshape(d, RET_HEADS, RET_DK)
    wk = w_in[:, RET_QK:2 * RET_QK].reshape(d, RET_HEADS, RET_DK)
    rot = lambda w: _rot_half(w, RET_DK // 2).reshape(d, RET_QK)
    w_ext = jnp.concatenate([w_in[:, :RET_QK], rot(wq), w_in[:, RET_QK:2 * RET_QK], rot(wk),
                             w_in[:, 2 * RET_QK:]], axis=1)
    return w_ext.astype(BF16)


def _rope_tables(positions):
    pos = positions.reshape(-1).astype(F32)[:, None]

    def cs(dim):
        inv = ROPE_BASE ** (-jnp.arange(0, dim, 2, dtype=F32) / dim)
        ang = pos * inv
        return jnp.cos(ang), jnp.sin(ang)

    t = pos.shape[0]
    cm, sm = cs(MLA_ROPE)
    pad = HEAD_PAD - MLA_NOPE - MLA_ROPE
    cos_m = jnp.concatenate([jnp.ones((t, MLA_NOPE), F32), cm, cm, jnp.zeros((t, pad), F32)], axis=1)
    sin_m = jnp.concatenate([jnp.zeros((t, MLA_NOPE), F32), sm, sm, jnp.zeros((t, pad), F32)], axis=1)
    cr, sr = cs(RET_DK)
    return cos_m, sin_m, jnp.concatenate([cr, cr], axis=1), jnp.concatenate([sr, sr], axis=1)


def _router_weights(rg_w, rg_b, re_w, re_b):
    d = rg_w.shape[0]
    wr_t = jnp.concatenate([rg_w.T, jnp.zeros((SUBLANES - N_GROUPS, d), F32), re_w.T], axis=0)
    br = jnp.concatenate([rg_b, jnp.full((SUBLANES - N_GROUPS,), NEG_BIG, F32), re_b])[:, None]
    return wr_t, br


def _moe_plan(cnt, n_tiles):
    cnt = cnt.astype(I32)
    ntile = (cnt + TM_EXPERT - 1) // TM_EXPERT
    psz = ntile * TM_EXPERT
    off = jnp.cumsum(psz) - psz
    tile_end = jnp.cumsum(ntile)
    nused = tile_end[-1:]
    tile = jnp.minimum(jnp.arange(n_tiles, dtype=I32), nused[0] - 1)
    te = jnp.sum((tile[:, None] >= tile_end[None, :]).astype(I32), axis=1)
    return off.astype(I32), psz.astype(I32), te.astype(I32), nused.astype(I32)


def kernel(x, c, positions, ada_mix_w, ada_mix_b, ln_mix_g, ln_mix_b, ab_w_in, ab_conv_w, ab_q_norm_g,
           ab_kv_norm_g, ab_w_uq, ab_w_ukv, ab_w_out, ret_w_in, ret_gn_g, ret_w_out, ada_ffn_w, ada_ffn_b,
           ln_ffn_g, ln_ffn_b, moe_rg_w, moe_rg_b, moe_re_w, moe_re_b, moe_w_gate, moe_w_up, moe_w_down):
    bsz, seq, d = x.shape
    t = bsz * seq
    assert d == D_MODEL and seq % TQ_ATTN == 0 and seq % TM_PROJ == 0 and seq % TM_MOE == 0
    n_rows = 2 * t + N_EXPERTS * TM_EXPERT
    n_tiles = n_rows // TM_EXPERT

    mod_mix = _adaln(c, ada_mix_w, ada_mix_b)
    mod_ffn = _adaln(c, ada_ffn_w, ada_ffn_b)
    split = lambda m: (m[:, None, :d], 1.0 + m[:, None, d:2 * d], 1.0 + m[:, None, 2 * d:])
    cos_m, sin_m, cos_r, sin_r = _rope_tables(positions)
    row = lambda v: v.reshape(1, -1)

    x2 = x.reshape(t, d)
    for l in range(DEPTH):
        shift, scale1p, gate1p = split(mod_mix[l])
        shift2, scale2_1p, gate2_1p = split(mod_ffn[l])
        wr_t, br = _router_weights(moe_rg_w[l], moe_rg_b[l], moe_re_w[l], moe_re_b[l])
        i = l // 2
        if l % 2 == 0:
            w_ext, wqa, wqb, wk, wv, wo_a, wo_b = _prep_ab(ab_w_in[i], ab_w_uq[i], ab_w_ukv[i], ab_w_out[i])
            ya, q, k, v = _ab_in(x2, shift, scale1p, w_ext, ab_conv_w[i], row(ab_q_norm_g[i]),
                                 row(ab_kv_norm_g[i]), wqa, wqb, wk, wv, cos_m, sin_m, seq)
            yb = _attn(q, k, v, bsz, seq)
            ys, ws = [ya, yb], [wo_a, wo_b]
        else:
            q, k, v, gs = _ret_in(x2, shift, scale1p, _prep_ret(ret_w_in[i]), cos_r, sin_r, seq)
            yr = _retention(q, k, v, gs, row(ret_gn_g[i]), bsz, seq)
            ys, ws = [yr], [ret_w_out[i].astype(BF16)]
        x2, ids, rank, info, cnt = _out_ln_route(ys, ws, x2, gate1p, row(ln_mix_g[l]), row(ln_mix_b[l]),
                                                 shift2, scale2_1p, wr_t, br, seq)
        off, psz, te, nused = _moe_plan(cnt[:, 0], n_tiles)
        xs, pos = _dispatch(off, psz, ids, rank, x2, shift2, scale2_1p, n_rows, seq)
        ysort = _experts(te, nused, xs, moe_w_gate[l].astype(BF16), moe_w_up[l].astype(BF16),
                         moe_w_down[l].astype(BF16))
        x2 = _combine_ln(pos, info, x2, gate2_1p, row(ln_ffn_g[l]), row(ln_ffn_b[l]), ysort, seq)
    return x2.reshape(bsz, seq, d)
```

```python
import functools

import numpy as np
import jax
import jax.numpy as jnp
from jax import lax
from jax.experimental import pallas as pl
from jax.experimental.pallas import tpu as pltpu

F32 = jnp.float32
BF16 = jnp.bfloat16
I32 = jnp.int32
HIGHEST = lax.Precision.HIGHEST

D_MODEL = 1024
DEPTH = 4
CONV_WIDTH = 512
CONV_K = 3
MLA_HEADS = 8
MLA_NOPE = 64
MLA_ROPE = 32
MLA_V = 64
MLA_Q_RANK = 256
MLA_KV_RANK = 128
ROPE_BASE = 10000.0
RET_HEADS = 8
RET_DK = 128
RET_DV = 256
RET_CHUNK = 128
N_GROUPS = 4
EXPERTS_PER_GROUP = 8
N_EXPERTS = 32
D_EXPERT = 256
LN_EPS = 1e-5
RMS_EPS = 1e-6
DEEPNORM_ALPHA = (2 * DEPTH) ** 0.25

LANES = 128
SUBLANES = 8
VMEM_LIMIT_BYTES = 56 * 2 ** 20
HEAD_PAD = LANES
TM_PROJ = 256
TM_ROUTE = 512
SUB_ROUTE = 256
TQ_ATTN = 512
TM_MOE = 256
TM_EXPERT = 256
ROUTER_ROWS = 40
NEG_BIG = -1e30


def _cparams(n_axes):
    return pltpu.CompilerParams(dimension_semantics=("arbitrary",) * n_axes,
                                vmem_limit_bytes=VMEM_LIMIT_BYTES)


def _silu(x):
    return x * (1.0 / (1.0 + jnp.exp(-x)))


def _adaln_kernel(c_ref, w_ref, b_ref, o_ref):
    c = c_ref[...]
    o_ref[0] = jnp.dot(_silu(c), w_ref[0], precision=HIGHEST, preferred_element_type=F32) + b_ref[0]


def _adaln(c, w, b):
    n_l, d, n3 = w.shape
    bsz = c.shape[0]
    tn = 1024
    return pl.pallas_call(
        _adaln_kernel,
        grid=(n_l, n3 // tn),
        in_specs=[pl.BlockSpec((bsz, d), lambda l, j: (0, 0)),
                  pl.BlockSpec((1, d, tn), lambda l, j: (l, 0, j)),
                  pl.BlockSpec((1, 1, tn), lambda l, j: (l, 0, j))],
        out_specs=pl.BlockSpec((1, bsz, tn), lambda l, j: (l, 0, j)),
        out_shape=jax.ShapeDtypeStruct((n_l, bsz, n3), F32),
        compiler_params=_cparams(2),
        name="adaln",
    )(c, w, b.reshape(n_l, 1, n3))


AB_COLS = 3 * CONV_WIDTH + MLA_Q_RANK + MLA_KV_RANK + 2 * LANES
AB_QC = 3 * CONV_WIDTH
AB_KVC = AB_QC + MLA_Q_RANK
AB_KR = AB_KVC + MLA_KV_RANK
AB_KRROT = AB_KR + LANES
N_HP = MLA_HEADS * HEAD_PAD


def _ab_in_kernel(x_ref, sh_ref, sc_ref, w_ref, cw_ref, qg_ref, kg_ref, wqa_ref, wqb_ref,
                  wk_ref, wv_ref, c_ref, s_ref, ya_ref, q_ref, k_ref, v_ref, carry_ref,
                  *, tm, tiles_per_seq):
    i = pl.program_id(0)
    h = (x_ref[...] * sc_ref[0] + sh_ref[0]).astype(BF16)
    p = jnp.dot(h, w_ref[...], preferred_element_type=F32)

    u = p[:, 2 * CONV_WIDTH:3 * CONV_WIDTH] * p[:, 0:CONV_WIDTH]
    first = (i % tiles_per_seq) == 0
    prev = jnp.where(first, 0.0, carry_ref[...])
    row = lax.broadcasted_iota(I32, u.shape, 0)
    p7 = prev[SUBLANES - 1:SUBLANES, :]
    p6 = prev[SUBLANES - 2:SUBLANES - 1, :]
    u1 = jnp.where(row == 0, p7, pltpu.roll(u, 1, 0))
    u2 = jnp.where(row == 0, p6, jnp.where(row == 1, p7, pltpu.roll(u, 2, 0)))
    carry_ref[...] = u[tm - SUBLANES:tm, :]
    cw = cw_ref[...]
    y = cw[0:1, :] * u2 + cw[1:2, :] * u1 + cw[2:3, :] * u
    ya_ref[...] = (p[:, CONV_WIDTH:2 * CONV_WIDTH] * y).astype(BF16)

    cos = c_ref[...]
    sin = s_ref[...]
    cos_h = jnp.concatenate([cos] * MLA_HEADS, axis=1)
    sin_h = jnp.concatenate([sin] * MLA_HEADS, axis=1)

    q_c = p[:, AB_QC:AB_KVC]
    qn = q_c * lax.rsqrt(jnp.mean(q_c * q_c, axis=1, keepdims=True) + RMS_EPS) * qg_ref[...]
    qn = qn.astype(BF16)
    qa = jnp.dot(qn, wqa_ref[...], preferred_element_type=F32)
    qb = jnp.dot(qn, wqb_ref[...], preferred_element_type=F32)
    q_ref[...] = (qa * cos_h + qb * sin_h).astype(BF16)

    kv_c = p[:, AB_KVC:AB_KR]
    kvn = kv_c * lax.rsqrt(jnp.mean(kv_c * kv_c, axis=1, keepdims=True) + RMS_EPS) * kg_ref[...]
    kvn = kvn.astype(BF16)
    kr = p[:, AB_KR:AB_KR + LANES] * cos + p[:, AB_KRROT:AB_KRROT + LANES] * sin
    kn = jnp.dot(kvn, wk_ref[...], preferred_element_type=F32)
    k_ref[...] = (kn + jnp.concatenate([kr] * MLA_HEADS, axis=1)).astype(BF16)
    v_ref[...] = jnp.dot(kvn, wv_ref[...], preferred_element_type=F32).astype(BF16)


def _ab_in(x2, shift, scale1p, w_ext, conv_w, qg, kg, wqa, wqb, wk, wv, cos128, sin128, seq):
    t, d = x2.shape
    tm = TM_PROJ
    tps = seq // tm
    full = lambda shp: pl.BlockSpec(shp, lambda i: (0,) * len(shp))
    mod = pl.BlockSpec((1, 1, d), lambda i: (i // tps, 0, 0))
    tok = lambda n: pl.BlockSpec((tm, n), lambda i: (i, 0))
    return pl.pallas_call(
        functools.partial(_ab_in_kernel, tm=tm, tiles_per_seq=tps),
        grid=(t // tm,),
        in_specs=[tok(d), mod, mod, full((d, AB_COLS)), full((CONV_K, CONV_WIDTH)),
                  full((1, MLA_Q_RANK)), full((1, MLA_KV_RANK)),
                  full((MLA_Q_RANK, N_HP)), full((MLA_Q_RANK, N_HP)),
                  full((MLA_KV_RANK, N_HP)), full((MLA_KV_RANK, N_HP)),
                  tok(LANES), tok(LANES)],
        out_specs=[tok(CONV_WIDTH), tok(N_HP), tok(N_HP), tok(N_HP)],
        out_shape=[jax.ShapeDtypeStruct((t, CONV_WIDTH), BF16)] + [jax.ShapeDtypeStruct((t, N_HP), BF16)] * 3,
        scratch_shapes=[pltpu.VMEM((SUBLANES, CONV_WIDTH), F32)],
        compiler_params=_cparams(1),
        name="ab_in",
    )(x2, shift, scale1p, w_ext, conv_w, qg, kg, wqa, wqb, wk, wv, cos128, sin128)


def _attn_kernel(q_ref, k_ref, v_ref, o_ref, *, tq):
    i = pl.program_id(1)
    nt = (((1,), (1,)), ((), ()))
    rows = lax.broadcasted_iota(I32, (tq, tq), 0)
    cols = lax.broadcasted_iota(I32, (tq, tq), 1)
    causal = cols <= rows

    def step(q, kb, vb, carry, mask):
        m, l, acc = carry
        s = lax.dot_general(q, kb, nt, preferred_element_type=F32)
        if mask:
            s = jnp.where(causal, s, -jnp.inf)
        m_new = jnp.maximum(m, jnp.max(s, axis=1, keepdims=True))
        a = jnp.exp(m - m_new)
        p = jnp.exp(s - m_new)
        l = a * l + jnp.sum(p, axis=1, keepdims=True)
        acc = a * acc + jnp.dot(p.astype(BF16), vb, preferred_element_type=F32)
        return m_new, l, acc

    for h in range(MLA_HEADS):
        lo = h * HEAD_PAD
        q = q_ref[:, lo:lo + HEAD_PAD]

        def body(j, carry, q=q, lo=lo):
            r0 = pl.multiple_of(j * tq, tq)
            return step(q, k_ref[pl.ds(r0, tq), lo:lo + HEAD_PAD], v_ref[pl.ds(r0, tq), lo:lo + HEAD_PAD],
                        carry, False)

        init = (jnp.full((tq, 1), -jnp.inf, F32), jnp.zeros((tq, 1), F32), jnp.zeros((tq, HEAD_PAD), F32))
        carry = lax.fori_loop(0, i, body, init)
        r0 = pl.multiple_of(i * tq, tq)
        m, l, acc = step(q, k_ref[pl.ds(r0, tq), lo:lo + HEAD_PAD], v_ref[pl.ds(r0, tq), lo:lo + HEAD_PAD],
                         carry, True)
        o_ref[:, lo:lo + HEAD_PAD] = (acc / l).astype(BF16)


def _attn(q, k, v, bsz, seq):
    tq = TQ_ATTN
    nq = seq // tq
    return pl.pallas_call(
        functools.partial(_attn_kernel, tq=tq),
        grid=(bsz, nq),
        in_specs=[pl.BlockSpec((tq, N_HP), lambda b, i: (b * nq + i, 0)),
                  pl.BlockSpec((seq, N_HP), lambda b, i: (b, 0)),
                  pl.BlockSpec((seq, N_HP), lambda b, i: (b, 0))],
        out_specs=pl.BlockSpec((tq, N_HP), lambda b, i: (b * nq + i, 0)),
        out_shape=jax.ShapeDtypeStruct(q.shape, BF16),
        compiler_params=_cparams(2),
        name="attn",
    )(q, k, v)


RET_QK = RET_HEADS * RET_DK
RET_V = RET_HEADS * RET_DV
RET_COLS = 2 * RET_QK + 2 * RET_V


def _ret_in_kernel(x_ref, sh_ref, sc_ref, w_ref, c_ref, s_ref, q_ref, k_ref, v_ref, g_ref):
    h = (x_ref[...] * sc_ref[0] + sh_ref[0]).astype(BF16)
    cos = c_ref[...]
    sin = s_ref[...]

    def rope(p):
        heads = []
        for hd in range(RET_HEADS):
            xh = p[:, hd * RET_DK:(hd + 1) * RET_DK]
            heads.append(xh * cos + pltpu.roll(xh, RET_DK // 2, 1) * sin)
        return jnp.concatenate(heads, axis=1)

    q_ref[...] = rope(jnp.dot(h, w_ref[:, 0:RET_QK], preferred_element_type=F32)).astype(BF16)
    pk = jnp.dot(h, w_ref[:, RET_QK:2 * RET_QK], preferred_element_type=F32)
    k_ref[...] = (rope(pk) * (RET_DK ** -0.5)).astype(BF16)
    v_ref[...] = jnp.dot(h, w_ref[:, 2 * RET_QK:2 * RET_QK + RET_V], preferred_element_type=F32).astype(BF16)
    g = jnp.dot(h, w_ref[:, 2 * RET_QK + RET_V:RET_COLS], preferred_element_type=F32)
    g_ref[...] = _silu(g).astype(BF16)


def _ret_in(x2, shift, scale1p, w_ext, cos128, sin128, seq):
    t, d = x2.shape
    tm = TM_PROJ
    tps = seq // tm
    mod = pl.BlockSpec((1, 1, d), lambda i: (i // tps, 0, 0))
    tok = lambda n: pl.BlockSpec((tm, n), lambda i: (i, 0))
    return pl.pallas_call(
        _ret_in_kernel,
        grid=(t // tm,),
        in_specs=[tok(d), mod, mod,
                  pl.BlockSpec((d, RET_COLS), lambda i: (0, 0), pipeline_mode=pl.Buffered(1)),
                  tok(LANES), tok(LANES)],
        out_specs=[tok(RET_QK), tok(RET_QK), tok(RET_V), tok(RET_V)],
        out_shape=[jax.ShapeDtypeStruct((t, RET_QK), BF16), jax.ShapeDtypeStruct((t, RET_QK), BF16),
                   jax.ShapeDtypeStruct((t, RET_V), BF16), jax.ShapeDtypeStruct((t, RET_V), BF16)],
        compiler_params=_cparams(1),
        name="ret_in",
    )(x2, shift, scale1p, w_ext, cos128, sin128)


def _retention_kernel(q_ref, k_ref, v_ref, g_ref, dm_ref, qd_ref, kd_ref, gn_ref, o_ref, st_ref, *, cdec):
    c = pl.program_id(1)

    @pl.when(c == 0)
    def _():
        st_ref[...] = jnp.zeros_like(st_ref)

    nt = (((1,), (1,)), ((), ()))
    for h in range(RET_HEADS):
        qh = q_ref[:, h * RET_DK:(h + 1) * RET_DK]
        kh = k_ref[:, h * RET_DK:(h + 1) * RET_DK]
        vh = v_ref[:, h * RET_DV:(h + 1) * RET_DV]
        st = st_ref[h]
        sc = lax.dot_general(qh, kh, nt, preferred_element_type=F32) * dm_ref[h]
        o = jnp.dot(sc.astype(BF16), vh, preferred_element_type=F32)
        o = o + qd_ref[h] * jnp.dot(qh, st.astype(BF16), preferred_element_type=F32)
        kt = (kh.astype(F32) * kd_ref[h]).T.astype(BF16)
        st_ref[h] = cdec[h] * st + jnp.dot(kt, vh, preferred_element_type=F32)
        mu = jnp.mean(o, axis=1, keepdims=True)
        oc = o - mu
        var = jnp.mean(oc * oc, axis=1, keepdims=True)
        y = oc * lax.rsqrt(var + LN_EPS) * gn_ref[:, h * RET_DV:(h + 1) * RET_DV]
        o_ref[:, h * RET_DV:(h + 1) * RET_DV] = (g_ref[:, h * RET_DV:(h + 1) * RET_DV].astype(F32) * y).astype(BF16)


def _retention_tables():
    hh = np.arange(RET_HEADS, dtype=np.float64)
    lg = np.log1p(-np.exp2(-5.0 - hh))
    n = np.arange(RET_CHUNK, dtype=np.float64)
    diff = n[:, None] - n[None, :]
    dm = np.where(diff >= 0, np.exp(lg[:, None, None] * np.maximum(diff, 0.0)), 0.0)
    qd = np.exp(lg[:, None] * (n + 1.0))[:, :, None]
    kd = np.exp(lg[:, None] * (RET_CHUNK - 1.0 - n))[:, :, None]
    cd = np.exp(lg * RET_CHUNK)
    return (jnp.asarray(dm, F32), jnp.asarray(qd, F32), jnp.asarray(kd, F32),
            tuple(float(np.float32(v)) for v in cd))


def _retention(q, k, v, gs, gn_g, bsz, seq):
    t = q.shape[0]
    nc = seq // RET_CHUNK
    dm, qd, kd, cdec = _retention_tables()
    tok = lambda n: pl.BlockSpec((RET_CHUNK, n), lambda b, c: (b * nc + c, 0))
    full = lambda shp: pl.BlockSpec(shp, lambda b, c: (0,) * len(shp))
    return pl.pallas_call(
        functools.partial(_retention_kernel, cdec=cdec),
        grid=(bsz, nc),
        in_specs=[tok(RET_QK), tok(RET_QK), tok(RET_V), tok(RET_V),
                  full((RET_HEADS, RET_CHUNK, RET_CHUNK)), full((RET_HEADS, RET_CHUNK, 1)),
                  full((RET_HEADS, RET_CHUNK, 1)), full((1, RET_V))],
        out_specs=tok(RET_V),
        out_shape=jax.ShapeDtypeStruct((t, RET_V), BF16),
        scratch_shapes=[pltpu.VMEM((RET_HEADS, RET_DK, RET_DV), F32)],
        compiler_params=_cparams(2),
        name="retention",
    )(q, k, v, gs, dm, qd, kd, gn_g)


def _layer_norm(z, g, b):
    mu = jnp.mean(z, axis=1, keepdims=True)
    zc = z - mu
    var = jnp.mean(zc * zc, axis=1, keepdims=True)
    return zc * lax.rsqrt(var + LN_EPS) * g + b


def _out_ln_route_kernel(*refs, n_y, tm, sub):
    y_refs = refs[:n_y]
    w_refs = refs[n_y:2 * n_y]
    (x_ref, g1_ref, lng_ref, lnb_ref, sh2_ref, sc2_ref, whi_ref, wlo_ref, br_ref,
     xo_ref, ids_ref, rank_ref, info_ref, cnt_ref, cnt_sc) = refs[2 * n_y:]
    i = pl.program_id(0)

    lgs = []
    for r in range(tm // sub):
        rs = pl.ds(r * sub, sub)
        y = jnp.dot(y_refs[0][rs, :], w_refs[0][...], preferred_element_type=F32)
        for yr, wr in zip(y_refs[1:], w_refs[1:]):
            y = y + jnp.dot(yr[rs, :], wr[...], preferred_element_type=F32)
        xn = _layer_norm(DEEPNORM_ALPHA * x_ref[rs, :] + g1_ref[0] * y, lng_ref[...], lnb_ref[...])
        xo_ref[rs, :] = xn
        h2 = xn * sc2_ref[0] + sh2_ref[0]
        h_hi = h2.astype(BF16)
        h_lo = (h2 - h_hi.astype(F32)).astype(BF16)
        lg = (jnp.dot(h_hi, whi_ref[...], preferred_element_type=F32)
              + jnp.dot(h_lo, whi_ref[...], preferred_element_type=F32)
              + jnp.dot(h_hi, wlo_ref[...], preferred_element_type=F32))
        lgs.append(lg.T)
    lg = jnp.concatenate(lgs, axis=1)[0:ROUTER_ROWS] + br_ref[...]
    r8 = lax.broadcasted_iota(I32, (SUBLANES, tm), 0)
    gl = lg[0:SUBLANES]
    ge = jnp.exp(gl - jnp.max(gl, axis=0, keepdims=True))
    gprob = ge / jnp.sum(ge, axis=0, keepdims=True)
    g_p = jnp.max(gprob, axis=0, keepdims=True)
    g_idx = jnp.min(jnp.where(gprob == g_p, r8, SUBLANES), axis=0, keepdims=True)
    el = lg[SUBLANES:ROUTER_ROWS]
    e_sel = el[3 * SUBLANES:4 * SUBLANES]
    for g in (2, 1, 0):
        e_sel = jnp.where(g_idx == g, el[g * SUBLANES:(g + 1) * SUBLANES], e_sel)
    ee = jnp.exp(e_sel - jnp.max(e_sel, axis=0, keepdims=True))
    eprob = ee / jnp.sum(ee, axis=0, keepdims=True)
    p1 = jnp.max(eprob, axis=0, keepdims=True)
    i1 = jnp.min(jnp.where(eprob == p1, r8, SUBLANES), axis=0, keepdims=True)
    rest = jnp.where(r8 == i1, -1.0, eprob)
    p2 = jnp.max(rest, axis=0, keepdims=True)
    i2 = jnp.min(jnp.where(rest == p2, r8, SUBLANES), axis=0, keepdims=True)
    den = p1 + p2
    w1 = g_p * p1 / den
    w2 = g_p * p2 / den
    id1 = g_idx * EXPERTS_PER_GROUP + i1
    id2 = g_idx * EXPERTS_PER_GROUP + i2

    @pl.when(i == 0)
    def _():
        cnt_sc[...] = jnp.zeros_like(cnt_sc)

    r32 = lax.broadcasted_iota(I32, (N_EXPERTS, tm), 0)
    is1 = r32 == id1
    is2 = r32 == id2
    onehot = jnp.where(is1 | is2, 1.0, 0.0)
    before = (lax.broadcasted_iota(I32, (tm, tm), 0) < lax.broadcasted_iota(I32, (tm, tm), 1))
    prefix = jnp.dot(onehot.astype(BF16), jnp.where(before, 1.0, 0.0).astype(BF16), preferred_element_type=F32)
    tot = prefix + cnt_sc[:, 0:1]
    rank1 = jnp.sum(jnp.where(is1, tot, 0.0), axis=0, keepdims=True)
    rank2 = jnp.sum(jnp.where(is2, tot, 0.0), axis=0, keepdims=True)
    cnt_sc[...] = cnt_sc[...] + jnp.sum(onehot, axis=1, keepdims=True)
    cnt_ref[...] = cnt_sc[...]

    ids_ref[...] = jnp.concatenate([id1, id2], axis=0)
    rank_ref[...] = jnp.concatenate([rank1, rank2], axis=0).astype(I32)
    packed = jnp.concatenate([w1, w2, jnp.zeros((LANES - 2, tm), F32)], axis=0)
    info_ref[...] = packed.T


def _out_ln_route(ys, ws, x2, gate1p, ln_g, ln_b, shift2, scale2_1p, w_hi, w_lo, br, seq):
    t, d = x2.shape
    tm = TM_ROUTE
    tps = seq // tm
    n_y = len(ys)
    full = lambda shp: pl.BlockSpec(shp, lambda i: (0,) * len(shp))
    mod = pl.BlockSpec((1, 1, d), lambda i: (i // tps, 0, 0))
    tok = lambda n: pl.BlockSpec((tm, n), lambda i: (i, 0))
    lane = pl.BlockSpec((2, tm), lambda i: (0, i))
    in_specs = ([tok(y.shape[1]) for y in ys] + [full(w.shape) for w in ws]
                + [tok(d), mod, full((1, d)), full((1, d)), mod, mod,
                   full((d, LANES)), full((d, LANES)), full((ROUTER_ROWS, 1))])
    return pl.pallas_call(
        functools.partial(_out_ln_route_kernel, n_y=n_y, tm=tm, sub=SUB_ROUTE),
        grid=(t // tm,),
        in_specs=in_specs,
        out_specs=[tok(d), lane, lane, tok(LANES), full((N_EXPERTS, LANES))],
        out_shape=[jax.ShapeDtypeStruct((t, d), F32), jax.ShapeDtypeStruct((2, t), I32),
                   jax.ShapeDtypeStruct((2, t), I32), jax.ShapeDtypeStruct((t, LANES), F32),
                   jax.ShapeDtypeStruct((N_EXPERTS, LANES), F32)],
        scratch_shapes=[pltpu.VMEM((N_EXPERTS, LANES), F32)],
        compiler_params=_cparams(1),
        name="out_ln_route",
    )(*ys, *ws, x2, gate1p, ln_g, ln_b, shift2, scale2_1p, w_hi, w_lo, br)


ROW_TILE = D_MODEL // LANES


def _rows_to_tiles(dst_ref, base, val, n):
    for s in range(ROW_TILE):
        dst_ref[pl.ds(base + s, n, stride=ROW_TILE), :] = val[:, s * LANES:(s + 1) * LANES]


def _tiles_to_rows(src_ref, base, n):
    return jnp.concatenate([src_ref[pl.ds(base + s, n, stride=ROW_TILE), :] for s in range(ROW_TILE)], axis=1)


def _tile_rows(ref, r, n=1):
    return ref.at[pl.ds(pl.multiple_of(r * ROW_TILE, ROW_TILE), n * ROW_TILE), :]


def _dispatch_kernel(off_ref, psz_ref, pos_ref, x_ref, sh_ref, sc_ref,
                     xs_ref, hbuf, zbuf, sem, zsem, *, tm, n_steps, n_tiles):
    i = pl.program_id(0)
    slot = i % 2

    def zero_copy(e):
        return pltpu.make_async_copy(zbuf, _tile_rows(xs_ref, off_ref[e] + psz_ref[e] - TM_EXPERT, TM_EXPERT), zsem)

    @pl.when(i == 0)
    def _():
        zbuf[...] = jnp.zeros_like(zbuf)
        for e in range(N_EXPERTS):
            @pl.when(psz_ref[e] > 0)
            def _(e=e):
                zero_copy(e).start()
        for e in range(N_EXPERTS):
            @pl.when(psz_ref[e] > 0)
            def _(e=e):
                zero_copy(e).wait()

        used = (off_ref[N_EXPERTS - 1] + psz_ref[N_EXPERTS - 1]) // TM_EXPERT

        def tail_copy(j):
            return pltpu.make_async_copy(zbuf, _tile_rows(xs_ref, j * TM_EXPERT, TM_EXPERT), zsem)

        def tail_start(j, carry):
            tail_copy(j).start()
            return carry

        def tail_wait(j, carry):
            tail_copy(j).wait()
            return carry

        lax.fori_loop(used, n_tiles, tail_start, 0)
        lax.fori_loop(used, n_tiles, tail_wait, 0)

    _rows_to_tiles(hbuf, slot * tm * ROW_TILE, x_ref[...] * sc_ref[0] + sh_ref[0], tm)

    def issue(t, carry):
        for k in range(2):
            pltpu.make_async_copy(_tile_rows(hbuf, slot * tm + t), _tile_rows(xs_ref, pos_ref[k, t]),
                                  sem.at[slot]).start(priority=k)
        return carry

    lax.fori_loop(0, tm, issue, 0, unroll=8)

    def drain(s):
        for _ in range(2):
            pltpu.make_async_copy(_tile_rows(hbuf, s * tm, tm), _tile_rows(xs_ref, 0, tm), sem.at[s]).wait()

    @pl.when(i > 0)
    def _():
        drain(1 - slot)

    @pl.when(i == n_steps - 1)
    def _():
        drain(slot)


def _dispatch(off, psz, pos, x2, shift2, scale2_1p, n_rows, seq):
    t, d = x2.shape
    tm = TM_MOE
    tps = seq // tm
    n_steps = t // tm
    smem = pl.BlockSpec((2, tm), lambda i, *_: (0, i), memory_space=pltpu.SMEM)
    mod = pl.BlockSpec((1, 1, d), lambda i, *_: (i // tps, 0, 0))
    grid_spec = pltpu.PrefetchScalarGridSpec(
        num_scalar_prefetch=2,
        grid=(n_steps,),
        in_specs=[smem, pl.BlockSpec((tm, d), lambda i, *_: (i, 0)), mod, mod],
        out_specs=pl.BlockSpec(memory_space=pl.ANY),
        scratch_shapes=[pltpu.VMEM((2 * tm * ROW_TILE, LANES), F32), pltpu.VMEM((TM_EXPERT * ROW_TILE, LANES), F32),
                        pltpu.SemaphoreType.DMA((2,)), pltpu.SemaphoreType.DMA],
    )
    return pl.pallas_call(
        functools.partial(_dispatch_kernel, tm=tm, n_steps=n_steps, n_tiles=n_rows // TM_EXPERT),
        grid_spec=grid_spec,
        out_shape=jax.ShapeDtypeStruct((n_rows * ROW_TILE, LANES), F32),
        compiler_params=_cparams(1),
        name="moe_dispatch",
    )(off, psz, pos, x2, shift2, scale2_1p)


def _experts_kernel(te_ref, nu_ref, xs_ref, wg_ref, wu_ref, wd_ref, ys_ref):
    @pl.when(pl.program_id(0) >= nu_ref[0])
    def _():
        ys_ref[...] = jnp.zeros_like(ys_ref)

    @pl.when(pl.program_id(0) < nu_ref[0])
    def _():
        x = _tiles_to_rows(xs_ref, 0, TM_EXPERT).astype(BF16)
        gate = jnp.dot(x, wg_ref[0].astype(BF16), preferred_element_type=F32)
        up = jnp.dot(x, wu_ref[0].astype(BF16), preferred_element_type=F32)
        y = jnp.dot((_silu(gate) * up).astype(BF16), wd_ref[0].astype(BF16), preferred_element_type=F32)
        _rows_to_tiles(ys_ref, 0, y, TM_EXPERT)


def _experts(te, nused, xs, wg, wu, wd):
    d = D_MODEL
    n_tiles = xs.shape[0] // (ROW_TILE * TM_EXPERT)
    row = lambda i, te_ref, nu_ref: (jnp.minimum(i, nu_ref[0] - 1), 0)
    grid_spec = pltpu.PrefetchScalarGridSpec(
        num_scalar_prefetch=2,
        grid=(n_tiles,),
        in_specs=[pl.BlockSpec((TM_EXPERT * ROW_TILE, LANES), row),
                  pl.BlockSpec((1, d, D_EXPERT), lambda i, te_ref, nu_ref: (te_ref[i], 0, 0)),
                  pl.BlockSpec((1, d, D_EXPERT), lambda i, te_ref, nu_ref: (te_ref[i], 0, 0)),
                  pl.BlockSpec((1, D_EXPERT, d), lambda i, te_ref, nu_ref: (te_ref[i], 0, 0))],
        out_specs=pl.BlockSpec((TM_EXPERT * ROW_TILE, LANES), lambda i, te_ref, nu_ref: (i, 0)),
    )
    return pl.pallas_call(
        _experts_kernel,
        grid_spec=grid_spec,
        out_shape=jax.ShapeDtypeStruct(xs.shape, F32),
        compiler_params=_cparams(1),
        name="moe_experts",
    )(te, nused, xs, wg, wu, wd)


def _combine_ln_kernel(pos_ref, posn_ref, info_ref, x_ref, g1_ref, lng_ref, lnb_ref, ys_ref, xo_ref, ybuf, sem,
                       *, tm, n_steps):
    i = pl.program_id(0)
    slot = i % 2

    def gather(p_ref, s):
        def issue(t, carry):
            for k in range(2):
                pltpu.make_async_copy(_tile_rows(ys_ref, p_ref[k, t]), _tile_rows(ybuf, (2 * s + k) * tm + t),
                                      sem.at[s]).start(priority=k)
            return carry

        lax.fori_loop(0, tm, issue, 0, unroll=8)

    @pl.when(i == 0)
    def _():
        gather(pos_ref, 0)

    @pl.when(i + 1 < n_steps)
    def _():
        gather(posn_ref, 1 - slot)

    for k in range(2):
        pltpu.make_async_copy(_tile_rows(ys_ref, 0, tm), _tile_rows(ybuf, (2 * slot + k) * tm, tm), sem.at[slot]).wait()

    info = info_ref[...]
    base = 2 * slot * tm * ROW_TILE
    y = (info[:, 0:1] * _tiles_to_rows(ybuf, base, tm)
         + info[:, 1:2] * _tiles_to_rows(ybuf, base + tm * ROW_TILE, tm))
    xo_ref[...] = _layer_norm(DEEPNORM_ALPHA * x_ref[...] + g1_ref[0] * y, lng_ref[...], lnb_ref[...])


def _combine_ln(pos, info, x2, gate1p, ln_g, ln_b, ys, seq):
    t, d = x2.shape
    tm = TM_MOE
    tps = seq // tm
    n_steps = t // tm
    full = lambda shp: pl.BlockSpec(shp, lambda i: (0,) * len(shp))
    return pl.pallas_call(
        functools.partial(_combine_ln_kernel, tm=tm, n_steps=n_steps),
        grid=(n_steps,),
        in_specs=[pl.BlockSpec((2, tm), lambda i: (0, i), memory_space=pltpu.SMEM),
                  pl.BlockSpec((2, tm), lambda i: (0, jnp.minimum(i + 1, n_steps - 1)), memory_space=pltpu.SMEM),
                  pl.BlockSpec((tm, LANES), lambda i: (i, 0)),
                  pl.BlockSpec((tm, d), lambda i: (i, 0)),
                  pl.BlockSpec((1, 1, d), lambda i: (i // tps, 0, 0)),
                  full((1, d)), full((1, d)),
                  pl.BlockSpec(memory_space=pl.ANY)],
        out_specs=pl.BlockSpec((tm, d), lambda i: (i, 0)),
        out_shape=jax.ShapeDtypeStruct((t, d), F32),
        scratch_shapes=[pltpu.VMEM((4 * tm * ROW_TILE, LANES), F32), pltpu.SemaphoreType.DMA((2,))],
        compiler_params=_cparams(1),
        name="moe_combine_ln",
    )(pos, pos, info, x2, gate1p, ln_g, ln_b, ys)


def _rot_half(w, half):
    return jnp.concatenate([-w[..., half:], w[..., :half]], axis=-1)


def _prep_ab(w_in, w_uq, w_ukv, w_out):
    d = w_in.shape[0]
    half = MLA_ROPE // 2
    kr = w_in[:, AB_KR:AB_KR + MLA_ROPE]
    z = lambda n: jnp.zeros((d, n), F32)
    w_ext = jnp.concatenate([w_in[:, :AB_KR], z(MLA_NOPE), kr, z(HEAD_PAD - MLA_NOPE - MLA_ROPE),
                             z(MLA_NOPE), _rot_half(kr, half), z(HEAD_PAD - MLA_NOPE - MLA_ROPE)], axis=1)
    scale = (MLA_NOPE + MLA_ROPE) ** -0.5
    wq = w_uq.reshape(MLA_Q_RANK, MLA_HEADS, MLA_NOPE + MLA_ROPE) * scale
    nope, rope = wq[..., :MLA_NOPE], wq[..., MLA_NOPE:]
    zq = lambda n: jnp.zeros((MLA_Q_RANK, MLA_HEADS, n), F32)
    pad = HEAD_PAD - MLA_NOPE - MLA_ROPE
    wqa = jnp.concatenate([nope, rope, zq(pad)], axis=-1).reshape(MLA_Q_RANK, N_HP)
    wqb = jnp.concatenate([zq(MLA_NOPE), _rot_half(rope, half), zq(pad)], axis=-1).reshape(MLA_Q_RANK, N_HP)
    wkv = w_ukv.reshape(MLA_KV_RANK, MLA_HEADS, MLA_NOPE + MLA_V)
    zk = jnp.zeros((MLA_KV_RANK, MLA_HEADS, HEAD_PAD - MLA_NOPE), F32)
    wk = jnp.concatenate([wkv[..., :MLA_NOPE], zk], axis=-1).reshape(MLA_KV_RANK, N_HP)
    wv = jnp.concatenate([wkv[..., MLA_NOPE:], zk], axis=-1).reshape(MLA_KV_RANK, N_HP)
    wo_b = w_out[CONV_WIDTH:].reshape(MLA_HEADS, MLA_V, d)
    wo_b = jnp.concatenate([wo_b, jnp.zeros((MLA_HEADS, HEAD_PAD - MLA_V, d), F32)], axis=1).reshape(N_HP, d)
    bf = lambda a: a.astype(BF16)
    return bf(w_ext), bf(wqa), bf(wqb), bf(wk), bf(wv), bf(w_out[:CONV_WIDTH]), bf(wo_b)


def _rope_tables(positions):
    pos = positions.reshape(-1).astype(F32)[:, None]

    def cs(dim):
        inv = ROPE_BASE ** (-jnp.arange(0, dim, 2, dtype=F32) / dim)
        ang = pos * inv
        return jnp.cos(ang), jnp.sin(ang)

    t = pos.shape[0]
    cm, sm = cs(MLA_ROPE)
    pad = HEAD_PAD - MLA_NOPE - MLA_ROPE
    cos_m = jnp.concatenate([jnp.ones((t, MLA_NOPE), F32), cm, cm, jnp.zeros((t, pad), F32)], axis=1)
    sin_m = jnp.concatenate([jnp.zeros((t, MLA_NOPE), F32), sm, sm, jnp.zeros((t, pad), F32)], axis=1)
    cr, sr = cs(RET_DK)
    return cos_m, sin_m, jnp.concatenate([cr, cr], axis=1), jnp.concatenate([-sr, sr], axis=1)


def _router_weights(rg_w, rg_b, re_w, re_b):
    d = rg_w.shape[0]
    w = jnp.concatenate([rg_w, jnp.zeros((d, SUBLANES - N_GROUPS), F32), re_w,
                         jnp.zeros((d, LANES - ROUTER_ROWS), F32)], axis=1)
    w_hi = w.astype(BF16)
    w_lo = (w - w_hi.astype(F32)).astype(BF16)
    br = jnp.concatenate([rg_b, jnp.full((SUBLANES - N_GROUPS,), NEG_BIG, F32), re_b])[:, None]
    return w_hi, w_lo, br


def _moe_plan(cnt, ids, rank, n_tiles):
    cnt = cnt.astype(I32)
    ntile = (cnt + TM_EXPERT - 1) // TM_EXPERT
    psz = ntile * TM_EXPERT
    off = jnp.cumsum(psz) - psz
    tile_end = jnp.cumsum(ntile)
    nused = tile_end[-1:]
    tile = jnp.minimum(jnp.arange(n_tiles, dtype=I32), nused[0] - 1)
    te = jnp.sum((tile[:, None] >= tile_end[None, :]).astype(I32), axis=1)
    e_iota = jnp.arange(N_EXPERTS, dtype=I32)
    pos = rank + jnp.sum(jnp.where(ids[..., None] == e_iota, off.astype(I32), 0), axis=-1)
    return off.astype(I32), psz.astype(I32), te.astype(I32), nused.astype(I32), pos.astype(I32)


def kernel(x, c, positions, ada_mix_w, ada_mix_b, ln_mix_g, ln_mix_b, ab_w_in, ab_conv_w, ab_q_norm_g,
           ab_kv_norm_g, ab_w_uq, ab_w_ukv, ab_w_out, ret_w_in, ret_gn_g, ret_w_out, ada_ffn_w, ada_ffn_b,
           ln_ffn_g, ln_ffn_b, moe_rg_w, moe_rg_b, moe_re_w, moe_re_b, moe_w_gate, moe_w_up, moe_w_down):
    bsz, seq, d = x.shape
    t = bsz * seq
    assert d == D_MODEL and all(seq % tile == 0 for tile in (TQ_ATTN, TM_PROJ, TM_MOE, TM_ROUTE, RET_CHUNK))
    n_rows = 2 * t + N_EXPERTS * TM_EXPERT
    n_tiles = n_rows // TM_EXPERT

    mod_mix = _adaln(c, ada_mix_w, ada_mix_b)
    mod_ffn = _adaln(c, ada_ffn_w, ada_ffn_b)
    split = lambda m: (m[:, None, :d], 1.0 + m[:, None, d:2 * d], 1.0 + m[:, None, 2 * d:])
    cos_m, sin_m, cos_r, sin_r = _rope_tables(positions)
    row = lambda v: v.reshape(1, -1)

    x2 = x.reshape(t, d)
    for l in range(DEPTH):
        shift, scale1p, gate1p = split(mod_mix[l])
        shift2, scale2_1p, gate2_1p = split(mod_ffn[l])
        wr_hi, wr_lo, br = _router_weights(moe_rg_w[l], moe_rg_b[l], moe_re_w[l], moe_re_b[l])
        i = l // 2
        if l % 2 == 0:
            w_ext, wqa, wqb, wk, wv, wo_a, wo_b = _prep_ab(ab_w_in[i], ab_w_uq[i], ab_w_ukv[i], ab_w_out[i])
            ya, q, k, v = _ab_in(x2, shift, scale1p, w_ext, ab_conv_w[i], row(ab_q_norm_g[i]),
                                 row(ab_kv_norm_g[i]), wqa, wqb, wk, wv, cos_m, sin_m, seq)
            yb = _attn(q, k, v, bsz, seq)
            ys, ws = [ya, yb], [wo_a, wo_b]
        else:
            q, k, v, gs = _ret_in(x2, shift, scale1p, ret_w_in[i].astype(BF16), cos_r, sin_r, seq)
            yr = _retention(q, k, v, gs, row(ret_gn_g[i]), bsz, seq)
            ys, ws = [yr], [ret_w_out[i].astype(BF16)]
        x2, ids, rank, info, cnt = _out_ln_route(ys, ws, x2, gate1p, row(ln_mix_g[l]), row(ln_mix_b[l]),
                                                 shift2, scale2_1p, wr_hi, wr_lo, br, seq)
        off, psz, te, nused, pos = _moe_plan(cnt[:, 0], ids, rank, n_tiles)
        xs = _dispatch(off, psz, pos, x2, shift2, scale2_1p, n_rows, seq)
        ysort = _experts(te, nused, xs, moe_w_gate[l], moe_w_up[l], moe_w_down[l])
        x2 = _combine_ln(pos, info, x2, gate2_1p, row(ln_ffn_g[l]), row(ln_ffn_b[l]), ysort, seq)
    return x2.reshape(bsz, seq, d)
```
